```python
import math
import jax
import jax.numpy as jnp
from jax import lax
import numpy as np

D_MODEL = 1024
BATCH = 2
SEQ = 8192
DEPTH = 4
DEC_BATCH = 32
DEC_SEQ = 1
PAST_LEN = 8192
PAGE_SIZE = 128

W_MIX = D_MODEL // 2
H_A = 4
DK_A = W_MIX // H_A
DV_A = W_MIX // H_A
HGRN_CHUNK = 64
H_B = 8
DH_B = W_MIX // H_B
G_B = 2
HPG_B = H_B // G_B
CMP_BLOCK = 32
CMP_HID = 2 * DH_B
SEL_BLOCK = 64
TOP_N = 16
WINDOW = 512
FORCE_SCORE = 1e4
H_C = 4
DH_C = W_MIX // (2 * H_C)
DV_C = 2 * DH_C
N_BUCKETS = 32
REL_MAX_DIST = 128
N_GROUPS = 4
EXP_PER_GROUP = 4
N_EXPERTS = N_GROUPS * EXP_PER_GROUP
D_FF_EXP = D_MODEL // 4
TOP_K_EXP = 2
Q_BLOCK = 128
EPS = 1e-6

IN_SPLITS = ([H_A * DK_A, H_A * DK_A, H_A * DV_A, H_A * DV_A, H_B * DH_B] + [G_B * DH_B] * 6
             + [3 * H_B, H_C * 2 * DH_C, H_C * 2 * DH_C, H_C * DV_C, 3 * D_MODEL])
N_IN = sum(IN_SPLITS)
IN_OFFSETS = [int(v) for v in np.cumsum(IN_SPLITS)[:-1]]

kernel_name = 'hybrid_hgrn2_nsa_diffattn_hmoe_step'


def rmsnorm(x, g):
    xf = x.astype(jnp.float32)
    y = xf * lax.rsqrt(jnp.mean(xf * xf, axis=-1, keepdims=True) + EPS)
    return (y * g.astype(jnp.float32)).astype(x.dtype)


def masked_softmax(s, mask):
    s = jnp.where(mask, s.astype(jnp.float32), -jnp.inf)
    m = jnp.max(s, axis=-1, keepdims=True)
    m = jnp.where(jnp.isfinite(m), m, 0.0)
    e = jnp.where(mask, jnp.exp(s - m), 0.0)
    den = jnp.sum(e, axis=-1, keepdims=True)
    return e / jnp.where(den > 0, den, 1.0)


def t5_bucket(dist):
    n = jnp.maximum(dist, 0)
    max_exact = N_BUCKETS // 2
    nf = jnp.maximum(n, 1).astype(jnp.float32)
    large = max_exact + (jnp.log(nf / max_exact) / math.log(REL_MAX_DIST / max_exact)
                         * (N_BUCKETS - max_exact)).astype(jnp.int32)
    return jnp.where(n < max_exact, n, jnp.minimum(large, N_BUCKETS - 1))


def in_proj(h, w):
    z = jnp.einsum('btd,dn->btn', h, w)
    return jnp.split(z, IN_OFFSETS, axis=-1)


def merge(o_a, o_b, o_c, m_g, w_br, w_out):
    o = jnp.stack([o_a, o_b, o_c], axis=-2)
    br = jnp.einsum('btnw,nwd->btnd', o, w_br)
    g = jax.nn.sigmoid(m_g.astype(jnp.float32)).reshape(br.shape)
    return jnp.einsum('btd,de->bte', jnp.sum(g * br, axis=-2), w_out)


def hgrn_inputs(a_q, a_f, a_i, lb):
    B, T = a_q.shape[:2]
    q = jax.nn.silu(a_q.astype(jnp.float32)).reshape(B, T, H_A, DK_A)
    lf = jnp.logaddexp(jnp.log(lb), jnp.log1p(-lb) + jax.nn.log_sigmoid(a_f.astype(jnp.float32)))
    lf = lf.reshape(B, T, H_A, DK_A)
    k = -jnp.expm1(lf)
    v = a_i.astype(jnp.float32).reshape(B, T, H_A, DV_A)
    return q, k, v, lf


def hgrn_chunk(S, inp):
    q, k, v, lf = inp
    C = q.shape[1]
    b = jnp.cumsum(lf, axis=1)
    o_inter = jnp.einsum('bthk,bhkv->bthv', q * jnp.exp(b), S)
    causal = jnp.tril(jnp.ones((C, C), dtype=bool))[None, :, :, None, None]
    dec = jnp.exp(jnp.where(causal, b[:, :, None] - b[:, None, :], -jnp.inf))
    a = jnp.sum(q[:, :, None] * k[:, None, :] * dec, axis=-1)
    o_intra = jnp.einsum('btsh,bshv->bthv', a, v)
    b_end = b[:, -1]
    S_new = (jnp.exp(b_end)[..., None] * S
             + jnp.einsum('bshk,bshv->bhkv', k * jnp.exp(b_end[:, None] - b), v))
    return S_new, o_inter + o_intra


def hgrn_out(o, a_g, g_norm):
    B, T = o.shape[:2]
    o = rmsnorm(o, g_norm) * jax.nn.silu(a_g.astype(jnp.float32)).reshape(B, T, H_A, DV_A)
    return o.reshape(B, T, W_MIX)


def compress(x, pe, w1, w2):
    B, L = x.shape[:2]
    nc = L // CMP_BLOCK
    xb = x[:, :nc * CMP_BLOCK].reshape(B, nc, CMP_BLOCK, G_B, DH_B) + pe[:, None, :]
    flat = xb.transpose(0, 1, 3, 2, 4).reshape(B, nc, G_B, CMP_BLOCK * DH_B)
    return jax.nn.silu(flat @ w1) @ w2


def nsa_block(q, q_pos, kc, vc, ks_blk, vs_blk, kw, vw, kw_pos, gates, bias_b):
    B, Q = q.shape[:2]
    qg = q.astype(jnp.float32).reshape(B, Q, G_B, HPG_B, DH_B) * (DH_B ** -0.5)
    nc = kc.shape[1]
    ns = ks_blk.shape[1]
    c_end = jnp.arange(nc) * CMP_BLOCK + (CMP_BLOCK - 1)
    vis_c = (c_end[None, :] <= q_pos[:, None])[None, :, None, None, :]
    p_c = masked_softmax(jnp.einsum('bqghd,bcgd->bqghc', qg, kc), vis_c)
    o_c = jnp.einsum('bqghc,bcgd->bqghd', p_c, vc)
    ratio = SEL_BLOCK // CMP_BLOCK
    imp = jnp.sum(p_c, axis=3)
    imp = jnp.pad(imp, ((0, 0), (0, 0), (0, 0), (0, ns * ratio - nc))).reshape(B, Q, G_B, ns, ratio).sum(-1)
    j = jnp.arange(ns)[None, :]
    qb = (q_pos // SEL_BLOCK)[:, None]
    valid = j * SEL_BLOCK <= q_pos[:, None]
    forced = (j == 0) | (j == qb) | (j == qb - 1)
    score = jnp.where(valid[None, :, None, :], jnp.where(forced[None, :, None, :], FORCE_SCORE, imp), -jnp.inf)
    _, idx = lax.top_k(score, min(TOP_N, ns))
    gather = jax.vmap(jax.vmap(lambda blk, ii: blk[ii]))
    idx_t = idx.transpose(0, 2, 1, 3)
    gk = gather(ks_blk.transpose(0, 3, 1, 2, 4), idx_t)
    gv = gather(vs_blk.transpose(0, 3, 1, 2, 4), idx_t)
    kpos = idx[..., None] * SEL_BLOCK + jnp.arange(SEL_BLOCK)
    bucket = t5_bucket(q_pos[None, :, None, None, None] - kpos)
    bias_g = bias_b.reshape(N_BUCKETS, G_B, HPG_B).transpose(1, 0, 2)
    b_sel = bias_g[jnp.arange(G_B)[None, None, :, None, None], bucket]
    s_s = jnp.einsum('bqghd,bgqnkd->bqghnk', qg, gk) + jnp.moveaxis(b_sel, -1, 3)
    m_s = (kpos <= q_pos[None, :, None, None, None])[:, :, :, None]
    p_s = masked_softmax(s_s.reshape(B, Q, G_B, HPG_B, -1), m_s.reshape(B, Q, G_B, 1, -1)).reshape(s_s.shape)
    o_s = jnp.einsum('bqghnk,bgqnkd->bqghd', p_s, gv)
    dist = q_pos[:, None] - kw_pos[None, :]
    m_w = (dist >= 0) & (dist <= WINDOW) & (kw_pos[None, :] >= 0)
    b_w = bias_b[t5_bucket(dist)].reshape(Q, -1, G_B, HPG_B).transpose(0, 2, 3, 1)
    p_w = masked_softmax(jnp.einsum('bqghd,bkgd->bqghk', qg, kw) + b_w, m_w[None, :, None, None, :])
    o_w = jnp.einsum('bqghk,bkgd->bqghd', p_w, vw)
    g = gates.reshape(B, Q, G_B, HPG_B, 3)
    o = g[..., 0:1] * o_c + g[..., 1:2] * o_s + g[..., 2:3] * o_w
    return o.reshape(B, Q, H_B * DH_B)


def diff_lambda_value(lp, layer):
    lam_init = 0.8 - 0.6 * math.exp(-0.3 * layer)
    lp = lp.astype(jnp.float32)
    lam = jnp.exp(jnp.sum(lp[0] * lp[1])) - jnp.exp(jnp.sum(lp[2] * lp[3])) + lam_init
    return lam, lam_init


def diff_block(q, q_pos, k, v, k_pos, lam, bias_c):
    s = jnp.einsum('bqhmd,bkhmd->bhmqk', q.astype(jnp.float32) * (DH_C ** -0.5), k)
    dist = q_pos[:, None] - k_pos[None, :]
    s = s + jnp.moveaxis(bias_c[t5_bucket(dist)], -1, 0)[None, :, None]
    p = masked_softmax(s, (dist >= 0)[None, None, None])
    w = p[:, :, 0] - lam * p[:, :, 1]
    return jnp.einsum('bhqk,bkhe->bqhe', w, v)


def diff_out(o, d_norm, lam_init):
    B, T = o.shape[:2]
    return (rmsnorm(o, d_norm) * (1.0 - lam_init)).reshape(B, T, W_MIX)


def hier_moe(h, w_grp, b_grp, w_er, b_er, w1, w3, w2):
    shp = h.shape
    x = h.reshape(-1, D_MODEL)
    gl = (x @ w_grp + b_grp).astype(jnp.float32)
    gp = jax.nn.softmax(gl, axis=-1)
    _, gi = lax.top_k(gl, 1)
    p_grp = jnp.take_along_axis(gp, gi, axis=-1)
    el = (x @ w_er + b_er).astype(jnp.float32).reshape(-1, N_GROUPS, EXP_PER_GROUP)
    el_g = jnp.take_along_axis(el, gi[:, :, None], axis=1)[:, 0]
    ev, ei = lax.top_k(el_g, TOP_K_EXP)
    w_sel = jax.nn.softmax(ev, axis=-1) * p_grp
    e_idx = gi * EXP_PER_GROUP + ei
    gate = jnp.sum(jax.nn.one_hot(e_idx, N_EXPERTS, dtype=jnp.float32) * w_sel[..., None], axis=1)
    u = jnp.einsum('nd,edf->nef', x, w1)
    v = jnp.einsum('nd,edf->nef', x, w3)
    a = jax.nn.silu(u) * v * gate[..., None]
    return jnp.einsum('nef,efd->nd', a, w2).reshape(shp)


def mixer_prompt(h, w_in_l, w_br_l, w_out_l, lb, hg_norm, phi_pe, phi_w1, phi_w2, lam, lam_init, d_norm, bias_b, bias_c):
    B, T, _ = h.shape
    (a_q, a_f, a_i, a_g, b_q, b_kc, b_vc, b_ks, b_vs, b_kw, b_vw, b_g, c_q, c_k, c_v, m_g) = in_proj(h, w_in_l)
    pos = jnp.arange(T, dtype=jnp.int32)
    n_qb = T // Q_BLOCK
    q, k, v, lf = hgrn_inputs(a_q, a_f, a_i, lb)

    def chunks(a):
        return a.reshape(B, T // HGRN_CHUNK, HGRN_CHUNK, *a.shape[2:]).swapaxes(0, 1)
    s0 = jnp.zeros((B, H_A, DK_A, DV_A), jnp.float32)
    s_fin, o = lax.scan(hgrn_chunk, s0, (chunks(q), chunks(k), chunks(v), chunks(lf)))
    o_a = hgrn_out(o.swapaxes(0, 1).reshape(B, T, H_A, DV_A), a_g, hg_norm)
    kc, vc, ks, vs, kw, vw = [a.reshape(B, T, G_B, DH_B) for a in (b_kc, b_vc, b_ks, b_vs, b_kw, b_vw)]
    kc_cmp = compress(kc, phi_pe[0], phi_w1[0], phi_w2[0])
    vc_cmp = compress(vc, phi_pe[1], phi_w1[1], phi_w2[1])
    ks_blk = ks.reshape(B, T // SEL_BLOCK, SEL_BLOCK, G_B, DH_B)
    vs_blk = vs.reshape(B, T // SEL_BLOCK, SEL_BLOCK, G_B, DH_B)
    pad_w = ((0, 0), (WINDOW, 0), (0, 0), (0, 0))
    kw_pad = jnp.pad(kw, pad_w)
    vw_pad = jnp.pad(vw, pad_w)
    qn = b_q.reshape(B, T, H_B, DH_B)
    gates = jax.nn.sigmoid(b_g.astype(jnp.float32)).reshape(B, T, H_B, 3)

    def nsa_step(i):
        q0 = i * Q_BLOCK
        qb = lax.dynamic_slice_in_dim(qn, q0, Q_BLOCK, axis=1)
        gb = lax.dynamic_slice_in_dim(gates, q0, Q_BLOCK, axis=1)
        kwb = lax.dynamic_slice_in_dim(kw_pad, q0, Q_BLOCK + WINDOW, axis=1)
        vwb = lax.dynamic_slice_in_dim(vw_pad, q0, Q_BLOCK + WINDOW, axis=1)
        q_pos = q0 + jnp.arange(Q_BLOCK, dtype=jnp.int32)
        kw_pos = q0 - WINDOW + jnp.arange(Q_BLOCK + WINDOW, dtype=jnp.int32)
        return nsa_block(qb, q_pos, kc_cmp, vc_cmp, ks_blk, vs_blk, kwb, vwb, kw_pos, gb, bias_b)
    o_b = lax.map(nsa_step, jnp.arange(n_qb, dtype=jnp.int32)).swapaxes(0, 1).reshape(B, T, W_MIX)
    win_rows = jnp.stack([kw, vw], axis=2)
    wb = min(WINDOW, PAST_LEN)
    if T >= wb:
        win_state = win_rows[:, T - wb:]
    else:
        win_state = jnp.pad(win_rows, ((0, 0), (wb - T, 0), (0, 0), (0, 0), (0, 0)))
    cq = c_q.reshape(B, T, H_C, 2, DH_C)
    ck = c_k.reshape(B, T, H_C, 2, DH_C)
    cv = c_v.reshape(B, T, H_C, DV_C)

    def diff_step(i):
        q0 = i * Q_BLOCK
        qb = lax.dynamic_slice_in_dim(cq, q0, Q_BLOCK, axis=1)
        return diff_block(qb, q0 + jnp.arange(Q_BLOCK, dtype=jnp.int32), ck, cv, pos, lam, bias_c)
    o = lax.map(diff_step, jnp.arange(n_qb, dtype=jnp.int32)).swapaxes(0, 1).reshape(B, T, H_C, DV_C)
    o_c = diff_out(o, d_norm, lam_init)
    y = merge(o_a, o_b, o_c, m_g, w_br_l, w_out_l)
    nsa_rows = jnp.stack([kc, vc, ks, vs], axis=2)
    return y, (nsa_rows, ck, cv, win_state, s_fin)


def mixer_sample(h, layer, cache_nsa, cache_diff_k, cache_diff_v, win_state, hg_state, page_table,
                 w_in_l, w_br_l, w_out_l, lb, hg_norm, phi_pe, phi_w1, phi_w2, lam, lam_init, d_norm, bias_b, bias_c):
    DB, S, _ = h.shape
    L = PAST_LEN + S
    q_pos = PAST_LEN + jnp.arange(S, dtype=jnp.int32)
    (a_q, a_f, a_i, a_g, b_q, b_kc, b_vc, b_ks, b_vs, b_kw, b_vw, b_g, c_q, c_k, c_v, m_g) = in_proj(h, w_in_l)
    q, k, v, lf = hgrn_inputs(a_q, a_f, a_i, lb)
    s_new, o = hgrn_chunk(hg_state.astype(jnp.float32), (q, k, v, lf))
    o_a = hgrn_out(o, a_g, hg_norm)
    kc, vc, ks, vs, kw, vw = [a.reshape(DB, S, G_B, DH_B) for a in (b_kc, b_vc, b_ks, b_vs, b_kw, b_vw)]
    new_rows = jnp.stack([kc, vc, ks, vs], axis=2)
    past = cache_nsa[layer, page_table].reshape(DB, PAST_LEN, 4, G_B, DH_B)
    rows = jnp.concatenate([past, new_rows], axis=1)
    kc_cmp = compress(rows[:, :, 0], phi_pe[0], phi_w1[0], phi_w2[0])
    vc_cmp = compress(rows[:, :, 1], phi_pe[1], phi_w1[1], phi_w2[1])
    ns = -(-L // SEL_BLOCK)
    sel = jnp.pad(rows[:, :, 2:4], ((0, 0), (0, ns * SEL_BLOCK - L), (0, 0), (0, 0), (0, 0)))
    sel = sel.reshape(DB, ns, SEL_BLOCK, 2, G_B, DH_B)
    wb = win_state.shape[1]
    win_all = jnp.concatenate([win_state, jnp.stack([kw, vw], axis=2)], axis=1)
    kw_pos = PAST_LEN - wb + jnp.arange(wb + S, dtype=jnp.int32)
    qn = b_q.reshape(DB, S, H_B, DH_B)
    gates = jax.nn.sigmoid(b_g.astype(jnp.float32)).reshape(DB, S, H_B, 3)
    o_b = nsa_block(qn, q_pos, kc_cmp, vc_cmp, sel[:, :, :, 0], sel[:, :, :, 1],
                    win_all[:, :, 0], win_all[:, :, 1], kw_pos, gates, bias_b)
    win_new = win_all[:, S:]
    cq = c_q.reshape(DB, S, H_C, 2, DH_C)
    ck = c_k.reshape(DB, S, H_C, 2, DH_C)
    cv = c_v.reshape(DB, S, H_C, DV_C)
    pk = cache_diff_k[layer, page_table].reshape(DB, PAST_LEN, H_C, 2, DH_C)
    pv = cache_diff_v[layer, page_table].reshape(DB, PAST_LEN, H_C, DV_C)
    k_all = jnp.concatenate([pk, ck], axis=1)
    v_all = jnp.concatenate([pv, cv], axis=1)
    o = diff_block(cq, q_pos, k_all, v_all, jnp.arange(L, dtype=jnp.int32), lam, bias_c)
    o_c = diff_out(o, d_norm, lam_init)
    y = merge(o_a, o_b, o_c, m_g, w_br_l, w_out_l)
    return y, (new_rows, ck, cv, win_new, s_new)


def setup_inputs(seed: int = 0) -> dict:
    key = jax.random.key(seed)
    ks = jax.random.split(key, 32)
    n_pages = PAST_LEN // PAGE_SIZE
    n_pool = (DEC_BATCH * n_pages * 5) // 4
    win_buf = min(WINDOW, PAST_LEN)
    nrm = jax.random.normal

    def w(k, shape, fan_in):
        return nrm(k, shape, jnp.float32) * (fan_in ** -0.5)

    def gain(k, shape):
        return 1.0 + 0.01 * nrm(k, shape, jnp.float32)
    page_table = jax.random.permutation(ks[7], n_pool)[:DEC_BATCH * n_pages].reshape(DEC_BATCH, n_pages).astype(jnp.int32)
    return {
        'x_prompt': nrm(ks[0], (BATCH, SEQ, D_MODEL), jnp.float32),
        'x_sample': nrm(ks[1], (DEC_BATCH, DEC_SEQ, D_MODEL), jnp.float32),
        'cache_nsa': nrm(ks[2], (DEPTH, n_pool, PAGE_SIZE, 4, G_B, DH_B), jnp.float32),
        'cache_diff_k': nrm(ks[3], (DEPTH, n_pool, PAGE_SIZE, H_C, 2, DH_C), jnp.float32),
        'cache_diff_v': nrm(ks[4], (DEPTH, n_pool, PAGE_SIZE, H_C, DV_C), jnp.float32),
        'state_nsa_win': nrm(ks[5], (DEPTH, DEC_BATCH, win_buf, 2, G_B, DH_B), jnp.float32),
        'state_hgrn': 0.5 * nrm(ks[6], (DEPTH, DEC_BATCH, H_A, DK_A, DV_A), jnp.float32),
        'page_table': page_table,
        'w_in': w(ks[8], (DEPTH, D_MODEL, N_IN), D_MODEL),
        'w_branch': w(ks[9], (DEPTH, 3, W_MIX, D_MODEL), W_MIX),
        'w_out': w(ks[10], (DEPTH, D_MODEL, D_MODEL), D_MODEL),
        'norm_attn': gain(ks[11], (DEPTH, D_MODEL)),
        'norm_ffn': gain(ks[12], (DEPTH, D_MODEL)),
        'norm_final': gain(ks[13], (D_MODEL,)),
        'hgrn_lb': nrm(ks[14], (DEPTH, H_A * DK_A), jnp.float32),
        'hgrn_norm': gain(ks[15], (DEPTH, DV_A)),
        'nsa_phi_pe': 0.1 * nrm(ks[16], (DEPTH, 2, CMP_BLOCK, DH_B), jnp.float32),
        'nsa_phi_w1': w(ks[17], (DEPTH, 2, CMP_BLOCK * DH_B, CMP_HID), CMP_BLOCK * DH_B),
        'nsa_phi_w2': w(ks[18], (DEPTH, 2, CMP_HID, DH_B), CMP_HID),
        'diff_lambda': 0.1 * nrm(ks[19], (DEPTH, 4, DH_C), jnp.float32),
        'diff_norm': gain(ks[20], (DEPTH, DV_C)),
        'rel_bias': 0.5 * nrm(ks[21], (N_BUCKETS, H_B + H_C), jnp.float32),
        'moe_w_group': w(ks[22], (DEPTH, D_MODEL, N_GROUPS), D_MODEL),
        'moe_b_group': 0.01 * nrm(ks[23], (DEPTH, N_GROUPS), jnp.float32),
        'moe_w_expert': w(ks[24], (DEPTH, D_MODEL, N_EXPERTS), D_MODEL),
        'moe_b_expert': 0.01 * nrm(ks[25], (DEPTH, N_EXPERTS), jnp.float32),
        'moe_w1': w(ks[26], (DEPTH, N_EXPERTS, D_MODEL, D_FF_EXP), D_MODEL),
        'moe_w3': w(ks[27], (DEPTH, N_EXPERTS, D_MODEL, D_FF_EXP), D_MODEL),
        'moe_w2': w(ks[28], (DEPTH, N_EXPERTS, D_FF_EXP, D_MODEL), D_FF_EXP),
    }


def reference(x_prompt, x_sample, cache_nsa, cache_diff_k, cache_diff_v, state_nsa_win, state_hgrn, page_table,
              w_in, w_branch, w_out, norm_attn, norm_ffn, norm_final, hgrn_lb, hgrn_norm,
              nsa_phi_pe, nsa_phi_w1, nsa_phi_w2, diff_lambda, diff_norm, rel_bias,
              moe_w_group, moe_b_group, moe_w_expert, moe_b_expert, moe_w1, moe_w3, moe_w2):
    lb_all = jnp.cumsum(jax.nn.softmax(hgrn_lb.astype(jnp.float32), axis=0), axis=0)
    lb_all = lb_all - lb_all[0:1]
    bias_b = rel_bias[:, :H_B].astype(jnp.float32)
    bias_c = rel_bias[:, H_B:].astype(jnp.float32)
    xp, xs = x_prompt, x_sample
    nsa_p, nsa_s, dk_p, dk_s, dv_p, dv_s, win_p, win_s, hg_p, hg_s = [], [], [], [], [], [], [], [], [], []
    for l in range(DEPTH):
        lam, lam_init = diff_lambda_value(diff_lambda[l], l)
        shared = (w_in[l], w_branch[l], w_out[l], lb_all[l], hgrn_norm[l], nsa_phi_pe[l], nsa_phi_w1[l],
                  nsa_phi_w2[l], lam, lam_init, diff_norm[l], bias_b, bias_c)
        moe_args = (moe_w_group[l], moe_b_group[l], moe_w_expert[l], moe_b_expert[l], moe_w1[l], moe_w3[l], moe_w2[l])
        y, st = mixer_prompt(rmsnorm(xp, norm_attn[l]), *shared)
        xp = xp + y
        xp = xp + hier_moe(rmsnorm(xp, norm_ffn[l]), *moe_args)
        y, st2 = mixer_sample(rmsnorm(xs, norm_attn[l]), l, cache_nsa, cache_diff_k, cache_diff_v,
                              state_nsa_win[l], state_hgrn[l], page_table, *shared)
        xs = xs + y
        xs = xs + hier_moe(rmsnorm(xs, norm_ffn[l]), *moe_args)
        nsa_p.append(st[0]); dk_p.append(st[1]); dv_p.append(st[2]); win_p.append(st[3]); hg_p.append(st[4])
        nsa_s.append(st2[0]); dk_s.append(st2[1]); dv_s.append(st2[2]); win_s.append(st2[3]); hg_s.append(st2[4])
    y_prompt = rmsnorm(xp, norm_final)
    y_sample = rmsnorm(xs, norm_final)
    return (y_prompt, y_sample,
            jnp.stack(nsa_p), jnp.stack(nsa_s), jnp.stack(dk_p), jnp.stack(dk_s),
            jnp.stack(dv_p), jnp.stack(dv_s), jnp.stack(win_p), jnp.stack(win_s),
            jnp.stack(hg_p), jnp.stack(hg_s))
```

```python
import functools
import math

import jax
import jax.numpy as jnp
import numpy as np
from jax import lax
from jax.experimental import pallas as pl
from jax.experimental.pallas import tpu as pltpu

F32 = jnp.float32
BF16 = jnp.bfloat16
HIGHEST = lax.Precision.HIGHEST

D_MODEL = 1024
DEPTH = 4
PAST_LEN = 8192
PAGE_SIZE = 128
W_MIX = D_MODEL // 2
H_A = 4
DK_A = W_MIX // H_A
DV_A = W_MIX // H_A
HGRN_CHUNK = 64
H_B = 8
DH_B = W_MIX // H_B
G_B = 2
HPG_B = H_B // G_B
CMP_BLOCK = 32
CMP_HID = 2 * DH_B
SEL_BLOCK = 64
TOP_N = 16
WINDOW = 512
FORCE_SCORE = 1e4
H_C = 4
DH_C = W_MIX // (2 * H_C)
DV_C = 2 * DH_C
N_BUCKETS = 32
REL_MAX_DIST = 128
N_GROUPS = 4
EXP_PER_GROUP = 4
N_EXPERTS = N_GROUPS * EXP_PER_GROUP
D_FF_EXP = D_MODEL // 4
TOP_K_EXP = 2
EPS = 1e-6

IN_SPLITS = ([H_A * DK_A, H_A * DK_A, H_A * DV_A, H_A * DV_A, H_B * DH_B] + [G_B * DH_B] * 6
             + [3 * H_B, H_C * 2 * DH_C, H_C * 2 * DH_C, H_C * DV_C, 3 * D_MODEL])
N_IN = sum(IN_SPLITS)

NEG = -1e30
LANES = 128
N_SEL_LANES = 128
Q_TILE = 128
K_TILE = 256
N_BIAS_TABLES = 4
VMEM_LIMIT = 56 * 1024 * 1024

OFF_ZA, OFF_BQ, OFF_ROWS, OFF_WIN, OFF_BG, OFF_CQ, OFF_CK, OFF_CV, OFF_MG = (
    0, 2048, 2560, 3072, 3328, 3456, 3968, 4480, 4992)
N_IN_PAD = OFF_MG + 3 * D_MODEL


def _in_proj_perm():
    offs = np.concatenate([[0], np.cumsum(IN_SPLITS)])
    o = {n: int(offs[i]) for i, n in enumerate(
        ['a_q', 'a_f', 'a_i', 'a_g', 'b_q', 'b_kc', 'b_vc', 'b_ks', 'b_vs', 'b_kw', 'b_vw', 'b_g', 'c_q', 'c_k',
         'c_v', 'm_g'])}
    perm = np.full((N_IN_PAD,), -1, np.int64)
    perm[OFF_ZA:OFF_ZA + 2048] = np.arange(2048)
    for hh in range(HPG_B):
        for g in range(G_B):
            dst = OFF_BQ + hh * 128 + g * 64
            perm[dst:dst + 64] = o['b_q'] + (g * HPG_B + hh) * DH_B + np.arange(DH_B)
    perm[OFF_ROWS:OFF_ROWS + 512] = o['b_kc'] + np.arange(512)
    perm[OFF_WIN:OFF_WIN + 256] = o['b_kw'] + np.arange(256)
    for j in range(3):
        for h in range(H_B):
            perm[OFF_BG + j * 8 + h] = o['b_g'] + h * 3 + j
    perm[OFF_CQ:OFF_CQ + 512] = o['c_q'] + np.arange(512)
    perm[OFF_CK:OFF_CK + 512] = o['c_k'] + np.arange(512)
    perm[OFF_CV:OFF_CV + 512] = o['c_v'] + np.arange(512)
    perm[OFF_MG:OFF_MG + 3072] = o['m_g'] + np.arange(3072)
    return perm


_IN_PERM = _in_proj_perm()
_NSA_OUT_PERM = np.array([(g * HPG_B + hh) * DH_B + d for hh in range(HPG_B) for g in range(G_B) for d in range(DH_B)])


def _dot(a, b):
    return jnp.dot(a, b, preferred_element_type=F32)


def _dot_nt(a, b):
    return lax.dot_general(a, b, (((1,), (1,)), ((), ())), preferred_element_type=F32)


def _dot_tn(a, b):
    return lax.dot_general(a, b, (((0,), (0,)), ((), ())), preferred_element_type=F32)


def _silu(x):
    return x * jax.nn.sigmoid(x)


def _cparams(*sem):
    return pltpu.CompilerParams(dimension_semantics=sem, vmem_limit_bytes=VMEM_LIMIT)


def _resident(shape, index_map):
    return pl.BlockSpec(shape, index_map, pipeline_mode=pl.Buffered(1))


def _inproj_body(x_ref, g_ref, w_ref, za_ref, bq_ref, rows_ref, win_ref, kv_ref, bg_ref, cq_ref, ck_ref, cv_ref,
                 ckb_ref, cvb_ref, mg_ref):
    x = x_ref[...]
    ms = jnp.mean(x * x, axis=-1, keepdims=True)
    h = (x * lax.rsqrt(ms + EPS) * g_ref[...]).astype(BF16)

    def mm(off, width):
        return _dot(h, w_ref[:, off:off + width])

    za_ref[...] = mm(OFF_ZA, 2048)
    bq_ref[...] = mm(OFF_BQ, 512).astype(BF16)
    rows = mm(OFF_ROWS, 512)
    rows_ref[...] = rows
    win = mm(OFF_WIN, 256)
    win_ref[...] = win
    kv_ref[...] = jnp.concatenate([rows[:, 256:], win], axis=1).astype(BF16)
    bg_ref[...] = mm(OFF_BG, 128)
    cq_ref[...] = mm(OFF_CQ, 512).astype(BF16)
    ck = mm(OFF_CK, 512)
    ck_ref[...] = ck
    ckb_ref[...] = ck.astype(BF16)
    cv = mm(OFF_CV, 512)
    cv_ref[...] = cv
    cvb_ref[...] = cv.astype(BF16)
    mg_ref[...] = mm(OFF_MG, 3072).astype(BF16)


def _in_proj(x, gain, w, tm):
    n = x.shape[0]
    widths = [(2048, F32), (512, BF16), (512, F32), (256, F32), (512, BF16), (128, F32), (512, BF16), (512, F32),
              (512, F32), (512, BF16), (512, BF16), (3072, BF16)]
    row = lambda i: (i, 0)
    return pl.pallas_call(
        _inproj_body,
        grid=(n // tm,),
        in_specs=[pl.BlockSpec((tm, D_MODEL), row),
                  pl.BlockSpec((1, D_MODEL), lambda i: (0, 0)),
                  _resident((D_MODEL, N_IN_PAD), lambda i: (0, 0))],
        out_specs=[pl.BlockSpec((tm, wd), row) for wd, _ in widths],
        out_shape=[jax.ShapeDtypeStruct((n, wd), dt) for wd, dt in widths],
        compiler_params=_cparams("parallel"),
        name="in_proj",
    )(x, gain, w)


_HG_LEVELS = (1, 2, 4, 8, 16, 32)


def _hgrn_consts():
    c = HGRN_CHUNK
    t = np.arange(c)
    tri = (t[None, :] <= t[:, None]).astype(np.float32)
    mats = [tri]
    masks = [np.eye(c, dtype=np.float32)]
    for m in _HG_LEVELS:
        r = 2 * m * (t // (2 * m)) + m - 1
        mats.append((t[None, :] <= r[:, None]).astype(np.float32))
        same = (t[:, None] // (2 * m)) == (t[None, :] // (2 * m))
        masks.append((same & ((t[None, :] % (2 * m)) < m) & ((t[:, None] % (2 * m)) >= m)).astype(np.float32))
    return np.concatenate(mats, axis=0), np.stack(masks)


_HG_MATS, _HG_MASKS = _hgrn_consts()
HG_BLOCK = 256


def _hgrn_gates(a_f, loglb, log1mlb, one_m_lb):
    ls = jnp.minimum(a_f, 0.0) - jnp.log1p(jnp.exp(-jnp.abs(a_f)))
    x2 = log1mlb + ls
    mx = jnp.maximum(loglb, x2)
    lf = mx + jnp.log1p(jnp.exp(-jnp.abs(loglb - x2)))
    k = one_m_lb * jax.nn.sigmoid(-a_f)
    return lf, k


def _hgrn_out_norm(o, gnorm, a_g):
    ms = jnp.mean(o * o, axis=-1, keepdims=True)
    return o * lax.rsqrt(ms + EPS) * gnorm * _silu(a_g)


def _hgrn_prompt_body(za_ref, lb_ref, gn_ref, mats_ref, masks_ref, o_ref, sfin_ref, st_ref):
    j = pl.program_id(1)

    @pl.when(j == 0)
    def _():
        st_ref[...] = jnp.zeros_like(st_ref)

    loglb, log1mlb, one_m_lb = lb_ref[0:1, :], lb_ref[1:2, :], lb_ref[2:3, :]
    c = HGRN_CHUNK

    def chunk(ci, carry):
        r0 = pl.multiple_of(ci * c, c)
        a_q = za_ref[0, pl.ds(r0, c), 0:512]
        a_f = za_ref[0, pl.ds(r0, c), 512:1024]
        a_i = za_ref[0, pl.ds(r0, c), 1024:1536]
        a_g = za_ref[0, pl.ds(r0, c), 1536:2048]
        q_all = _silu(a_q)
        lf, k_all = _hgrn_gates(a_f, loglb, log1mlb, one_m_lb)
        ball = jnp.dot(mats_ref[...], lf, precision=HIGHEST, preferred_element_type=F32)
        b_all = ball[0:c]
        outs = []
        for h in range(H_A):
            sl = slice(h * 128, (h + 1) * 128)
            q, kk, v, b = q_all[:, sl], k_all[:, sl], a_i[:, sl], b_all[:, sl]
            a = masks_ref[0] * _dot_nt(q.astype(BF16), kk.astype(BF16))
            for li in range(len(_HG_LEVELS)):
                r = ball[(li + 1) * c:(li + 2) * c, sl]
                ql = (q * jnp.exp(jnp.minimum(b - r, 0.0))).astype(BF16)
                kl = (kk * jnp.exp(jnp.minimum(r - b, 0.0))).astype(BF16)
                a = a + masks_ref[li + 1] * _dot_nt(ql, kl)
            vb = v.astype(BF16)
            st = st_ref[h]
            o = _dot_nt((q * jnp.exp(b)).astype(BF16), st.astype(BF16)) + _dot(a.astype(BF16), vb)
            b_end = b[c - 1:c, :]
            kd = (kk * jnp.exp(b_end - b)).astype(BF16)
            st_ref[h] = jnp.exp(b_end) * st + _dot_tn(vb, kd)
            outs.append(_hgrn_out_norm(o, gn_ref[...], a_g[:, sl]))
        o_ref[0, pl.ds(r0, c), :] = jnp.concatenate(outs, axis=1).astype(BF16)
        return carry

    lax.fori_loop(0, HG_BLOCK // c, chunk, 0)

    @pl.when(j == pl.num_programs(1) - 1)
    def _():
        for h in range(H_A):
            sfin_ref[0, h] = st_ref[h].T


def _hgrn_prompt(za, lb3, gnorm, bsz, t):
    za3 = za.reshape(bsz, t, 2048)
    nmat = _HG_MATS.shape[0]
    return pl.pallas_call(
        _hgrn_prompt_body,
        grid=(bsz, t // HG_BLOCK),
        in_specs=[pl.BlockSpec((1, HG_BLOCK, 2048), lambda b, j: (b, j, 0)),
                  pl.BlockSpec((3, W_MIX), lambda b, j: (0, 0)),
                  pl.BlockSpec((1, DV_A), lambda b, j: (0, 0)),
                  pl.BlockSpec((nmat, HGRN_CHUNK), lambda b, j: (0, 0)),
                  pl.BlockSpec((len(_HG_LEVELS) + 1, HGRN_CHUNK, HGRN_CHUNK), lambda b, j: (0, 0, 0))],
        out_specs=[pl.BlockSpec((1, HG_BLOCK, W_MIX), lambda b, j: (b, j, 0)),
                   pl.BlockSpec((1, H_A, DK_A, DV_A), lambda b, j: (b, 0, 0, 0))],
        out_shape=[jax.ShapeDtypeStruct((bsz, t, W_MIX), BF16),
                   jax.ShapeDtypeStruct((bsz, H_A, DK_A, DV_A), F32)],
        scratch_shapes=[pltpu.VMEM((H_A, DV_A, DK_A), F32)],
        compiler_params=_cparams("parallel", "arbitrary"),
        name="hgrn_prompt",
    )(za3, lb3, gnorm, jnp.asarray(_HG_MATS), jnp.asarray(_HG_MASKS))


def _compress_rows(x_refs, pe_ref, w1_ref, w2_ref, n_blk, emit):
    for kv in range(2):
        for par in range(2):
            def step(i, acc):
                xi = x_refs[kv][pl.ds(par * CMP_BLOCK + i, n_blk, stride=SEL_BLOCK), :]
                xi = (xi + pe_ref[kv, pl.ds(i, 1), :]).astype(BF16)
                return acc + _dot(xi, w1_ref[kv, i])
            acc = lax.fori_loop(0, CMP_BLOCK, step, jnp.zeros((n_blk, 2 * CMP_HID), F32))
            emit(kv, par, _dot(_silu(acc).astype(BF16), w2_ref[kv]))


def _compress_prompt_body(xk_ref, xv_ref, pe_ref, w1_ref, w2_ref, o_ref, *, n_blk):
    def emit(kv, par, val):
        if n_blk < N_SEL_LANES:
            val = jnp.concatenate([val, jnp.zeros((N_SEL_LANES - n_blk, 128), F32)], axis=0)
        o_ref[0, 2 * kv + par] = val.astype(BF16)
    _compress_rows((xk_ref, xv_ref), pe_ref, w1_ref, w2_ref, n_blk, emit)


def _compress_prompt(rows, pe, w1, w2, bsz, t):
    n_blk = t // SEL_BLOCK
    return pl.pallas_call(
        functools.partial(_compress_prompt_body, n_blk=n_blk),
        grid=(bsz,),
        in_specs=[pl.BlockSpec((t, 128), lambda b: (b, 0)),
                  pl.BlockSpec((t, 128), lambda b: (b, 1)),
                  pl.BlockSpec((2, CMP_BLOCK, 128), lambda b: (0, 0, 0)),
                  pl.BlockSpec((2, CMP_BLOCK, 128, 2 * CMP_HID), lambda b: (0, 0, 0, 0)),
                  pl.BlockSpec((2, 2 * CMP_HID, 128), lambda b: (0, 0, 0))],
        out_specs=pl.BlockSpec((1, 4, N_SEL_LANES, 128), lambda b: (b, 0, 0, 0)),
        out_shape=jax.ShapeDtypeStruct((bsz, 4, N_SEL_LANES, 128), BF16),
        compiler_params=_cparams("parallel"),
        name="nsa_compress_prompt",
    )(rows, rows, pe, w1, w2)


def _nsa_lhs(q):
    rows = q.shape[0]
    lane = lax.broadcasted_iota(jnp.int32, (rows, 128), 1)
    parts = []
    for g in range(G_B):
        for hh in range(HPG_B):
            slab = q[:, hh * 128:(hh + 1) * 128].astype(F32) * (DH_B ** -0.5)
            parts.append(jnp.where((lane // DH_B) == g, slab, 0.0))
    return jnp.concatenate(parts, axis=0).astype(BF16)


def _top_blocks(score, lane, n_rounds):
    def rnd(_, carry):
        sc, sel = carry
        mx = jnp.max(sc, axis=-1, keepdims=True)
        first = jnp.min(jnp.where(sc == mx, lane, N_SEL_LANES), axis=-1, keepdims=True)
        pick = lane == first
        return jnp.where(pick, -jnp.inf, sc), jnp.where(pick, 1.0, sel)
    _, sel = lax.fori_loop(0, n_rounds, rnd, (score, jnp.zeros_like(score)))
    return sel


def _flash_update(s, v, m_ref, l_ref, acc_ref):
    m_prev = m_ref[...]
    m_new = jnp.maximum(m_prev, jnp.max(s, axis=-1, keepdims=True))
    alpha = jnp.exp(m_prev - m_new)
    p = jnp.exp(s - jnp.concatenate([m_new] * (s.shape[1] // LANES), axis=1))
    l_ref[...] = alpha * l_ref[...] + jnp.sum(p, axis=-1, keepdims=True)
    acc_ref[...] = alpha * acc_ref[...] + _dot(p.astype(BF16), v)
    m_ref[...] = m_new


def _flash_init(m_ref, l_ref, acc_ref):
    m_ref[...] = jnp.full(m_ref.shape, NEG, F32)
    l_ref[...] = jnp.zeros(l_ref.shape, F32)
    acc_ref[...] = jnp.zeros(acc_ref.shape, F32)


def _nsa_prompt_body(q_ref, bg_ref, cmp_ref, kv_ref, tab_ref, o_ref, m_ref, l_ref, acc_ref, *, n_rounds):
    i = pl.program_id(1)
    q0 = i * Q_TILE
    nrow = H_B * Q_TILE
    lhs = _nsa_lhs(q_ref[0])

    ke, ko, ve, vo = cmp_ref[0, 0], cmp_ref[0, 1], cmp_ref[0, 2], cmp_ref[0, 3]
    row = lax.broadcasted_iota(jnp.int32, (nrow, N_SEL_LANES), 0)
    blk = lax.broadcasted_iota(jnp.int32, (nrow, N_SEL_LANES), 1)
    t = q0 + (row & (Q_TILE - 1))
    vis_e = (SEL_BLOCK * blk + (CMP_BLOCK - 1)) <= t
    vis_o = (SEL_BLOCK * blk + (SEL_BLOCK - 1)) <= t
    se = jnp.where(vis_e, _dot_nt(lhs, ke), NEG)
    so = jnp.where(vis_o, _dot_nt(lhs, ko), NEG)
    mx = jnp.maximum(jnp.max(se, axis=-1, keepdims=True), jnp.max(so, axis=-1, keepdims=True))
    ee = jnp.where(vis_e, jnp.exp(se - mx), 0.0)
    eo = jnp.where(vis_o, jnp.exp(so - mx), 0.0)
    den = jnp.sum(ee, axis=-1, keepdims=True) + jnp.sum(eo, axis=-1, keepdims=True)
    inv = 1.0 / jnp.where(den > 0, den, 1.0)
    pe_, po = ee * inv, eo * inv
    o_c = _dot(pe_.astype(BF16), ve) + _dot(po.astype(BF16), vo)
    ptot = pe_ + po

    t1 = q0 + lax.broadcasted_iota(jnp.int32, (Q_TILE, N_SEL_LANES), 0)
    jj = lax.broadcasted_iota(jnp.int32, (Q_TILE, N_SEL_LANES), 1)
    qb = t1 // SEL_BLOCK
    valid = (jj * SEL_BLOCK) <= t1
    forced = (jj == 0) | (jj == qb) | (jj == qb - 1)
    sels = []
    for g in range(G_B):
        imp = ptot[(g * HPG_B) * Q_TILE:(g * HPG_B + 1) * Q_TILE]
        for hh in range(1, HPG_B):
            imp = imp + ptot[(g * HPG_B + hh) * Q_TILE:(g * HPG_B + hh + 1) * Q_TILE]
        score = jnp.where(valid, jnp.where(forced, FORCE_SCORE, imp), -jnp.inf)
        sels.append(jnp.where(valid, _top_blocks(score, jj, n_rounds), 0.0))
    sel_bf = jnp.concatenate(sels, axis=0).astype(BF16)

    n_tiles = (q0 + Q_TILE + K_TILE - 1) // K_TILE
    per_blk = K_TILE // SEL_BLOCK

    _flash_init(m_ref, l_ref, acc_ref)

    def sel_step(kt, carry):
        k0 = pl.multiple_of(kt * K_TILE, K_TILE)
        ks = kv_ref[0, pl.ds(k0, K_TILE), 0:128]
        vs = kv_ref[0, pl.ds(k0, K_TILE), 128:256]
        s = _dot_nt(lhs, ks) + tab_ref[jnp.minimum((q0 - k0) // Q_TILE, N_BIAS_TABLES - 1)]
        eb = lax.broadcasted_iota(jnp.int32, (N_SEL_LANES, K_TILE), 0)
        ek = lax.broadcasted_iota(jnp.int32, (N_SEL_LANES, K_TILE), 1)
        expand = jnp.where(eb == kt * per_blk + ek // SEL_BLOCK, 1.0, 0.0).astype(BF16)
        madd = (_dot(sel_bf, expand) - 1.0) * (-NEG)
        s = s + jnp.concatenate([madd[:Q_TILE]] * HPG_B + [madd[Q_TILE:]] * HPG_B, axis=0)
        _flash_update(s, vs, m_ref, l_ref, acc_ref)
        return carry

    lax.fori_loop(0, n_tiles, sel_step, 0)
    o_s = acc_ref[...] / l_ref[...]

    _flash_init(m_ref, l_ref, acc_ref)

    def win_step(kt, carry):
        k0 = pl.multiple_of(kt * K_TILE, K_TILE)
        kw = kv_ref[0, pl.ds(k0, K_TILE), 256:384]
        vw = kv_ref[0, pl.ds(k0, K_TILE), 384:512]
        s = _dot_nt(lhs, kw) + tab_ref[jnp.minimum((q0 - k0) // Q_TILE, N_BIAS_TABLES - 1)]
        wr = lax.broadcasted_iota(jnp.int32, (nrow, K_TILE), 0)
        wk = lax.broadcasted_iota(jnp.int32, (nrow, K_TILE), 1)
        dist = q0 + (wr & (Q_TILE - 1)) - (k0 + wk)
        s = s + jnp.where(dist > WINDOW, NEG, 0.0)
        _flash_update(s, vw, m_ref, l_ref, acc_ref)
        return carry

    lax.fori_loop(jnp.maximum(q0 - WINDOW, 0) // K_TILE, n_tiles, win_step, 0)
    o_w = acc_ref[...] / l_ref[...]

    gates = jax.nn.sigmoid(bg_ref[0])

    def gate_col(j):
        return jnp.concatenate(
            [jnp.broadcast_to(gates[:, j * 8 + h:j * 8 + h + 1], (Q_TILE, 128)) for h in range(H_B)], axis=0)

    o = gate_col(0) * o_c + gate_col(1) * o_s + gate_col(2) * o_w
    lane = lax.broadcasted_iota(jnp.int32, (Q_TILE, 128), 1)
    outs = []
    for hh in range(HPG_B):
        o_g0 = o[hh * Q_TILE:(hh + 1) * Q_TILE]
        o_g1 = o[(HPG_B + hh) * Q_TILE:(HPG_B + hh + 1) * Q_TILE]
        outs.append(jnp.where(lane < DH_B, o_g0, o_g1))
    o_ref[0] = jnp.concatenate(outs, axis=1).astype(BF16)


def _nsa_prompt(bq, bg, cmp, kv, tab, bsz, t):
    nrow = H_B * Q_TILE
    return pl.pallas_call(
        functools.partial(_nsa_prompt_body, n_rounds=min(TOP_N, t // SEL_BLOCK)),
        grid=(bsz, t // Q_TILE),
        in_specs=[pl.BlockSpec((1, Q_TILE, W_MIX), lambda b, i: (b, i, 0)),
                  pl.BlockSpec((1, Q_TILE, 128), lambda b, i: (b, i, 0)),
                  pl.BlockSpec((1, 4, N_SEL_LANES, 128), lambda b, i: (b, 0, 0, 0)),
                  _resident((1, t, 512), lambda b, i: (b, 0, 0)),
                  _resident((N_BIAS_TABLES, nrow, K_TILE), lambda b, i: (0, 0, 0))],
        out_specs=pl.BlockSpec((1, Q_TILE, W_MIX), lambda b, i: (b, i, 0)),
        out_shape=jax.ShapeDtypeStruct((bsz, t, W_MIX), BF16),
        scratch_shapes=[pltpu.VMEM((nrow, 128), F32), pltpu.VMEM((nrow, 128), F32), pltpu.VMEM((nrow, 128), F32)],
        compiler_params=_cparams("parallel", "arbitrary"),
        name="nsa_prompt",
    )(bq.reshape(bsz, t, W_MIX), bg.reshape(bsz, t, 128), cmp, kv.reshape(bsz, t, 512), tab)


def _diff_lhs(q):
    lane = lax.broadcasted_iota(jnp.int32, q.shape, 1)
    qs = q * (DH_C ** -0.5)
    z = jnp.zeros_like(qs)
    return jnp.concatenate([jnp.where(lane < DH_C, qs, z), jnp.where(lane >= DH_C, qs, z)], axis=0)


def _diff_finish(o1, o2, lam, dn, out_scale):
    o = o1 - lam * o2
    ms = jnp.mean(o * o, axis=-1, keepdims=True)
    return o * lax.rsqrt(ms + EPS) * dn * out_scale


def _diff_prompt_body(lam_ref, q_ref, k_ref, v_ref, tab_ref, dn_ref, o_ref, m_ref, l_ref, acc_ref, *, out_scale):
    i = pl.program_id(1)
    q0 = i * Q_TILE
    lhs = [_diff_lhs(q_ref[0, :, h * 128:(h + 1) * 128]) for h in range(H_C)]
    for h in range(H_C):
        _flash_init(m_ref.at[h], l_ref.at[h], acc_ref.at[h])

    def step(kt, carry):
        k0 = pl.multiple_of(kt * K_TILE, K_TILE)
        ti = jnp.minimum((q0 - k0) // Q_TILE, N_BIAS_TABLES - 1)
        for h in range(H_C):
            kk = k_ref[0, pl.ds(k0, K_TILE), h * 128:(h + 1) * 128]
            vv = v_ref[0, pl.ds(k0, K_TILE), h * 128:(h + 1) * 128]
            tb = tab_ref[ti, h]
            s = _dot_nt(lhs[h], kk) + jnp.concatenate([tb, tb], axis=0)
            _flash_update(s, vv, m_ref.at[h], l_ref.at[h], acc_ref.at[h])
        return carry

    lax.fori_loop(0, (q0 + Q_TILE + K_TILE - 1) // K_TILE, step, 0)
    lam = lam_ref[0]
    outs = []
    for h in range(H_C):
        o = acc_ref[h] / l_ref[h]
        outs.append(_diff_finish(o[:Q_TILE], o[Q_TILE:], lam, dn_ref[...], out_scale))
    o_ref[0] = jnp.concatenate(outs, axis=1).astype(BF16)


def _diff_prompt(lam, cq, ckb, cvb, tab, dn, out_scale, bsz, t):
    return pl.pallas_call(
        functools.partial(_diff_prompt_body, out_scale=out_scale),
        grid_spec=pltpu.PrefetchScalarGridSpec(
            num_scalar_prefetch=1,
            grid=(bsz, t // Q_TILE),
            in_specs=[pl.BlockSpec((1, Q_TILE, W_MIX), lambda b, i, lam: (b, i, 0)),
                      _resident((1, t, W_MIX), lambda b, i, lam: (b, 0, 0)),
                      _resident((1, t, W_MIX), lambda b, i, lam: (b, 0, 0)),
                      _resident((N_BIAS_TABLES, H_C, Q_TILE, K_TILE), lambda b, i, lam: (0, 0, 0, 0)),
                      pl.BlockSpec((1, DV_C), lambda b, i, lam: (0, 0))],
            out_specs=pl.BlockSpec((1, Q_TILE, W_MIX), lambda b, i, lam: (b, i, 0)),
            scratch_shapes=[pltpu.VMEM((H_C, 2 * Q_TILE, 128), F32)] * 3),
        out_shape=jax.ShapeDtypeStruct((bsz, t, W_MIX), BF16),
        compiler_params=_cparams("parallel", "arbitrary"),
        name="diff_prompt",
    )(lam, cq.reshape(bsz, t, W_MIX), ckb.reshape(bsz, t, W_MIX), cvb.reshape(bsz, t, W_MIX), tab, dn)


def _merge_body(x_ref, oa_ref, ob_ref, oc_ref, mg_ref, wbr_ref, wout_ref, o_ref):
    acc = None
    for n, oref in enumerate((oa_ref, ob_ref, oc_ref)):
        br = _dot(oref[...], wbr_ref[n])
        term = jax.nn.sigmoid(mg_ref[:, n * D_MODEL:(n + 1) * D_MODEL].astype(F32)) * br
        acc = term if acc is None else acc + term
    o_ref[...] = x_ref[...] + _dot(acc.astype(BF16), wout_ref[...])


def _merge(x, oa, ob, oc, mg, wbr, wout, tm):
    n = x.shape[0]
    row = lambda i: (i, 0)
    return pl.pallas_call(
        _merge_body,
        grid=(n // tm,),
        in_specs=[pl.BlockSpec((tm, D_MODEL), row), pl.BlockSpec((tm, W_MIX), row), pl.BlockSpec((tm, W_MIX), row),
                  pl.BlockSpec((tm, W_MIX), row), pl.BlockSpec((tm, 3 * D_MODEL), row),
                  _resident((3, W_MIX, D_MODEL), lambda i: (0, 0, 0)),
                  _resident((D_MODEL, D_MODEL), lambda i: (0, 0))],
        out_specs=pl.BlockSpec((tm, D_MODEL), row),
        out_shape=jax.ShapeDtypeStruct((n, D_MODEL), F32),
        compiler_params=_cparams("parallel"),
        name="merge",
    )(x, oa, ob, oc, mg, wbr, wout)


def _moe_body(x_ref, g_ref, wr_ref, br_ref, w1_ref, w3_ref, w2_ref, gf_ref, o_ref, y_ref):
    x = x_ref[...]
    ms = jnp.mean(x * x, axis=-1, keepdims=True)
    hf = x * lax.rsqrt(ms + EPS) * g_ref[...]
    h = hf.astype(BF16)
    h_lo = (hf - h.astype(F32)).astype(BF16)
    logit = (_dot(h, wr_ref[0]) + _dot(h_lo, wr_ref[0]) + _dot(h, wr_ref[1])) + br_ref[...]
    lane = lax.broadcasted_iota(jnp.int32, logit.shape, 1)
    is_g = lane < N_GROUPS
    gmx = jnp.max(jnp.where(is_g, logit, -jnp.inf), axis=-1, keepdims=True)
    gi = jnp.min(jnp.where(is_g & (logit == gmx), lane, LANES), axis=-1, keepdims=True)
    p_grp = 1.0 / jnp.sum(jnp.where(is_g, jnp.exp(logit - gmx), 0.0), axis=-1, keepdims=True)
    in_grp = (lane >= N_GROUPS) & (lane < N_GROUPS + N_EXPERTS) & (((lane - N_GROUPS) // EXP_PER_GROUP) == gi)
    v1 = jnp.max(jnp.where(in_grp, logit, -jnp.inf), axis=-1, keepdims=True)
    i1 = jnp.min(jnp.where(in_grp & (logit == v1), lane, LANES), axis=-1, keepdims=True)
    rest = in_grp & (lane != i1)
    v2 = jnp.max(jnp.where(rest, logit, -jnp.inf), axis=-1, keepdims=True)
    i2 = jnp.min(jnp.where(rest & (logit == v2), lane, LANES), axis=-1, keepdims=True)
    e2 = jnp.exp(v2 - v1)
    w_1 = p_grp / (1.0 + e2)
    w_2 = p_grp * e2 / (1.0 + e2)
    gate = jnp.where(lane == i1, w_1, jnp.where(lane == i2, w_2, 0.0))
    tm = x.shape[0]
    acc = jnp.zeros((tm, D_MODEL), F32)
    for grp in range(N_GROUPS):
        cs = slice(grp * EXP_PER_GROUP * D_FF_EXP, (grp + 1) * EXP_PER_GROUP * D_FF_EXP)
        u = _dot(h, w1_ref[:, cs])
        v = _dot(h, w3_ref[:, cs])
        gexp = jnp.concatenate(
            [jnp.broadcast_to(gate[:, N_GROUPS + grp * EXP_PER_GROUP + e:N_GROUPS + grp * EXP_PER_GROUP + e + 1],
                              (tm, D_FF_EXP)) for e in range(EXP_PER_GROUP)], axis=1)
        a = _silu(u) * v * gexp
        acc = acc + _dot(a.astype(BF16), w2_ref[cs, :])
    out = x + acc
    o_ref[...] = out
    ms2 = jnp.mean(out * out, axis=-1, keepdims=True)
    y_ref[...] = out * lax.rsqrt(ms2 + EPS) * gf_ref[...]


def _moe(x, gain, wr, br, w1, w3, w2, gfinal, tm):
    n = x.shape[0]
    row = lambda i: (i, 0)
    nf = N_EXPERTS * D_FF_EXP
    return pl.pallas_call(
        _moe_body,
        grid=(n // tm,),
        in_specs=[pl.BlockSpec((tm, D_MODEL), row),
                  pl.BlockSpec((1, D_MODEL), lambda i: (0, 0)),
                  _resident((2, D_MODEL, 128), lambda i: (0, 0, 0)),
                  pl.BlockSpec((1, 128), lambda i: (0, 0)),
                  _resident((D_MODEL, nf), lambda i: (0, 0)),
                  _resident((D_MODEL, nf), lambda i: (0, 0)),
                  _resident((nf, D_MODEL), lambda i: (0, 0)),
                  pl.BlockSpec((1, D_MODEL), lambda i: (0, 0))],
        out_specs=[pl.BlockSpec((tm, D_MODEL), row), pl.BlockSpec((tm, D_MODEL), row)],
        out_shape=[jax.ShapeDtypeStruct((n, D_MODEL), F32), jax.ShapeDtypeStruct((n, D_MODEL), F32)],
        compiler_params=_cparams("parallel"),
        name="moe",
    )(x, gain, wr, br, w1, w3, w2, gfinal)


DEC_PAGES = PAST_LEN // PAGE_SIZE
PAGES_PER_STEP = 8
N_PAST_BLOCKS = PAST_LEN // SEL_BLOCK
N_SEL_PAST = TOP_N - 1


def _row_to_col(x_row):
    n = x_row.shape[1]
    eye = lax.broadcasted_iota(jnp.int32, (n, n), 0) == lax.broadcasted_iota(jnp.int32, (n, n), 1)
    return jnp.sum(jnp.where(eye, jnp.broadcast_to(x_row, (n, n)), 0.0), axis=1, keepdims=True)


def _hgrn_sample_body(za_ref, s_ref, lb_ref, gn_ref, o_ref, sn_ref):
    za = za_ref[0]
    a_q, a_f, a_i, a_g = za[:, 0:512], za[:, 512:1024], za[:, 1024:1536], za[:, 1536:2048]
    q_all = _silu(a_q)
    lf, k_all = _hgrn_gates(a_f, lb_ref[0:1, :], lb_ref[1:2, :], lb_ref[2:3, :])
    f_all = jnp.exp(lf)
    outs = []
    for h in range(H_A):
        sl = slice(h * 128, (h + 1) * 128)
        s_new = _row_to_col(f_all[:, sl]) * s_ref[0, h] + _row_to_col(k_all[:, sl]) * a_i[:, sl]
        sn_ref[0, h] = s_new
        o = jnp.sum(_row_to_col(q_all[:, sl]) * s_new, axis=0, keepdims=True)
        outs.append(_hgrn_out_norm(o, gn_ref[...], a_g[:, sl]))
    o_ref[0] = jnp.concatenate(outs, axis=1).astype(BF16)


def _hgrn_sample(za, state, lb3, gnorm):
    nseq = za.shape[0]
    return pl.pallas_call(
        _hgrn_sample_body,
        grid=(nseq,),
        in_specs=[pl.BlockSpec((1, 1, 2048), lambda s: (s, 0, 0)),
                  pl.BlockSpec((1, H_A, DK_A, DV_A), lambda s: (s, 0, 0, 0)),
                  pl.BlockSpec((3, W_MIX), lambda s: (0, 0)),
                  pl.BlockSpec((1, DV_A), lambda s: (0, 0))],
        out_specs=[pl.BlockSpec((1, 1, W_MIX), lambda s: (s, 0, 0)),
                   pl.BlockSpec((1, H_A, DK_A, DV_A), lambda s: (s, 0, 0, 0))],
        out_shape=[jax.ShapeDtypeStruct((nseq, 1, W_MIX), BF16),
                   jax.ShapeDtypeStruct((nseq, H_A, DK_A, DV_A), F32)],
        compiler_params=_cparams("parallel"),
        name="hgrn_sample",
    )(za.reshape(nseq, 1, 2048), state, lb3, gnorm)


def _nsa_sample_cmp_body(pt_ref, *refs):
    np_ = PAGES_PER_STEP
    k_pages, v_pages = refs[:np_], refs[np_:2 * np_]
    q_ref, pe_ref, w1_ref, w2_ref, oc_ref, idx_ref, xk_ref, xv_ref = refs[2 * np_:]
    p = pl.program_id(1)
    for j in range(np_):
        r0 = pl.multiple_of((p * np_ + j) * PAGE_SIZE, PAGE_SIZE)
        xk_ref[pl.ds(r0, PAGE_SIZE), :] = k_pages[j][...]
        xv_ref[pl.ds(r0, PAGE_SIZE), :] = v_pages[j][...]

    @pl.when(p == pl.num_programs(1) - 1)
    def _():
        cmp = {}

        def emit(kv, par, val):
            cmp[(kv, par)] = val.astype(BF16)
        _compress_rows((xk_ref, xv_ref), pe_ref, w1_ref, w2_ref, N_PAST_BLOCKS, emit)
        lhs = _nsa_lhs(q_ref[0])
        se = _dot_nt(lhs, cmp[(0, 0)])
        so = _dot_nt(lhs, cmp[(0, 1)])
        mx = jnp.maximum(jnp.max(se, axis=-1, keepdims=True), jnp.max(so, axis=-1, keepdims=True))
        ee, eo = jnp.exp(se - mx), jnp.exp(so - mx)
        inv = 1.0 / (jnp.sum(ee, axis=-1, keepdims=True) + jnp.sum(eo, axis=-1, keepdims=True))
        pe_, po = ee * inv, eo * inv
        oc_ref[0] = _dot(pe_.astype(BF16), cmp[(1, 0)]) + _dot(po.astype(BF16), cmp[(1, 1)])
        ptot = pe_ + po
        imps = [jnp.sum(ptot[g * HPG_B:(g + 1) * HPG_B], axis=0, keepdims=True) for g in range(G_B)]
        imp = jnp.concatenate(imps + [jnp.zeros((8 - G_B, N_SEL_LANES), F32)], axis=0)
        lane = lax.broadcasted_iota(jnp.int32, imp.shape, 1)
        forced = (lane == 0) | (lane == N_PAST_BLOCKS - 1)
        score = jnp.where(forced, FORCE_SCORE, imp)

        def rnd(r, carry):
            sc, idx = carry
            m = jnp.max(sc, axis=-1, keepdims=True)
            first = jnp.min(jnp.where(sc == m, lane, N_SEL_LANES), axis=-1, keepdims=True)
            return jnp.where(lane == first, -jnp.inf, sc), jnp.where(lane == r, first, idx)
        _, idx = lax.fori_loop(0, N_SEL_PAST, rnd, (score, jnp.zeros(imp.shape, jnp.int32)))
        idx_ref[0] = idx


def _nsa_sample_cmp(page_table, cache4, l, bq, pe, w1, w2):
    nseq = bq.shape[0]
    np_ = PAGES_PER_STEP

    def page_spec(j, col):
        return pl.BlockSpec((None, None, PAGE_SIZE, 128), lambda s, p, pt: (l, pt[s, p * np_ + j], 0, col))
    return pl.pallas_call(
        _nsa_sample_cmp_body,
        grid_spec=pltpu.PrefetchScalarGridSpec(
            num_scalar_prefetch=1,
            grid=(nseq, DEC_PAGES // np_),
            in_specs=([page_spec(j, 0) for j in range(np_)] + [page_spec(j, 1) for j in range(np_)]
                      + [pl.BlockSpec((1, 1, W_MIX), lambda s, p, pt: (s, 0, 0)),
                         pl.BlockSpec((2, CMP_BLOCK, 128), lambda s, p, pt: (0, 0, 0)),
                         pl.BlockSpec((2, CMP_BLOCK, 128, 2 * CMP_HID), lambda s, p, pt: (0, 0, 0, 0)),
                         pl.BlockSpec((2, 2 * CMP_HID, 128), lambda s, p, pt: (0, 0, 0))]),
            out_specs=[pl.BlockSpec((1, H_B, 128), lambda s, p, pt: (s, 0, 0)),
                       pl.BlockSpec((1, 8, N_SEL_LANES), lambda s, p, pt: (s, 0, 0))],
            scratch_shapes=[pltpu.VMEM((PAST_LEN, 128), F32), pltpu.VMEM((PAST_LEN, 128), F32)]),
        out_shape=[jax.ShapeDtypeStruct((nseq, H_B, 128), F32),
                   jax.ShapeDtypeStruct((nseq, 8, N_SEL_LANES), jnp.int32)],
        compiler_params=_cparams("parallel", "arbitrary"),
        name="nsa_sample_cmp",
    )(page_table, *([cache4] * (2 * np_)), bq.reshape(nseq, 1, W_MIX), pe, w1, w2)


def _small_flash(s, v, m, l, acc):
    m_new = jnp.maximum(m, jnp.max(s, axis=-1, keepdims=True))
    alpha = jnp.exp(m - m_new)
    p = jnp.exp(s - m_new[:, 0:1])
    return m_new, alpha * l + jnp.sum(p, axis=-1, keepdims=True), alpha * acc + _dot(p.astype(BF16), v)


def _nsa_sample_att_body(pg_ref, hf_ref, jb_ref, *refs):
    nb = G_B * N_SEL_PAST
    blocks = refs[:nb]
    q_ref, bg_ref, oc_ref, new_ref, win_ref, bblk_ref, wb_ref, nb_ref, o_ref = refs[nb:]
    s_id = pl.program_id(0)
    lhs = _nsa_lhs(q_ref[0])
    lhs_f = lhs.astype(F32)
    new = new_ref[0].astype(F32)
    row = lax.broadcasted_iota(jnp.int32, (H_B, SEL_BLOCK), 0)

    m = jnp.broadcast_to(jnp.sum(lhs_f * new[:, 0:128], axis=1, keepdims=True), (H_B, 128)) + nb_ref[...]
    l = jnp.ones((H_B, 128), F32)
    acc = jnp.broadcast_to(new[:, 128:256], (H_B, 128))
    for g in range(G_B):
        for r in range(N_SEL_PAST):
            blk = blocks[g * N_SEL_PAST + r]
            kb = blk[:, 0:128].astype(BF16)
            vb = blk[:, 128:256].astype(BF16)
            s = _dot_nt(lhs, kb) + bblk_ref[jb_ref[s_id, g * N_SEL_PAST + r]]
            s = jnp.where((row // HPG_B) == g, s, NEG)
            m, l, acc = _small_flash(s, vb, m, l, acc)
    o_s = acc / l

    kw = win_ref[0, :, 0:128].astype(BF16)
    vw = win_ref[0, :, 128:256].astype(BF16)
    m = jnp.broadcast_to(jnp.sum(lhs_f * new[:, 256:384], axis=1, keepdims=True), (H_B, 128)) + nb_ref[...]
    l = jnp.ones((H_B, 128), F32)
    acc = jnp.broadcast_to(new[:, 384:512], (H_B, 128))
    m, l, acc = _small_flash(_dot_nt(lhs, kw) + wb_ref[...], vw, m, l, acc)
    o_w = acc / l

    gates = jnp.broadcast_to(jax.nn.sigmoid(bg_ref[0]), (H_B, 128))
    grow = lax.broadcasted_iota(jnp.int32, (H_B, 128), 0)
    glane = lax.broadcasted_iota(jnp.int32, (H_B, 128), 1)

    def gate_col(j):
        return jnp.sum(jnp.where(glane == j * H_B + grow, gates, 0.0), axis=1, keepdims=True)
    o = gate_col(0) * oc_ref[0] + gate_col(1) * o_s + gate_col(2) * o_w
    lane1 = lax.broadcasted_iota(jnp.int32, (1, 128), 1)
    outs = [jnp.where(lane1 < DH_B, o[hh:hh + 1], o[HPG_B + hh:HPG_B + hh + 1]) for hh in range(HPG_B)]
    o_ref[0] = jnp.concatenate(outs, axis=1).astype(BF16)


def _nsa_sample_att(pages, halves, jblk, cache5, l, bq, bg, oc, kvnew, win_state, bblk, wbias, nbias):
    nseq = bq.shape[0]
    nb = G_B * N_SEL_PAST

    def blk_spec(n):
        return pl.BlockSpec((None, None, None, SEL_BLOCK, 256),
                            lambda s, pg, hf, jb: (l, pg[s, n], hf[s, n], 0, 1))
    c3 = lambda s, pg, hf, jb: (s, 0, 0)
    return pl.pallas_call(
        _nsa_sample_att_body,
        grid_spec=pltpu.PrefetchScalarGridSpec(
            num_scalar_prefetch=3,
            grid=(nseq,),
            in_specs=([blk_spec(n) for n in range(nb)]
                      + [pl.BlockSpec((1, 1, W_MIX), c3), pl.BlockSpec((1, 1, 128), c3),
                         pl.BlockSpec((1, H_B, 128), c3), pl.BlockSpec((1, 1, 512), c3),
                         pl.BlockSpec((1, WINDOW, 256), c3),
                         pl.BlockSpec((N_PAST_BLOCKS, H_B, SEL_BLOCK), lambda s, pg, hf, jb: (0, 0, 0)),
                         pl.BlockSpec((H_B, WINDOW), lambda s, pg, hf, jb: (0, 0)),
                         pl.BlockSpec((H_B, 128), lambda s, pg, hf, jb: (0, 0))]),
            out_specs=pl.BlockSpec((1, 1, W_MIX), c3)),
        out_shape=jax.ShapeDtypeStruct((nseq, 1, W_MIX), BF16),
        compiler_params=_cparams("arbitrary"),
        name="nsa_sample_att",
    )(pages, halves, jblk, *([cache5] * nb), bq.reshape(nseq, 1, W_MIX), bg.reshape(nseq, 1, 128), oc,
      kvnew.reshape(nseq, 1, 512), win_state, bblk, wbias, nbias)


def _diff_sample_body(pt_ref, lam_ref, *refs, out_scale):
    np_ = PAGES_PER_STEP
    k_pages, v_pages = refs[:np_], refs[np_:2 * np_]
    q_ref, knew_ref, vnew_ref, bpage_ref, nb_ref, dn_ref, o_ref, m_ref, l_ref, acc_ref = refs[2 * np_:]
    p = pl.program_id(1)
    nrow = 2 * H_C
    q = jnp.broadcast_to(q_ref[0].astype(F32), (nrow, W_MIX)) * (DH_C ** -0.5)
    row = lax.broadcasted_iota(jnp.int32, (nrow, W_MIX), 0)
    lane = lax.broadcasted_iota(jnp.int32, (nrow, W_MIX), 1)
    lhs = jnp.where((lane // DH_C) == row, q, 0.0).astype(BF16)

    @pl.when(p == 0)
    def _():
        s_new = jnp.sum(lhs.astype(F32) * knew_ref[0].astype(F32), axis=1, keepdims=True)
        m_ref[...] = jnp.broadcast_to(s_new, (nrow, 128)) + nb_ref[...]
        l_ref[...] = jnp.ones((nrow, 128), F32)
        acc_ref[...] = jnp.broadcast_to(vnew_ref[0].astype(F32), (nrow, W_MIX))

    for j in range(np_):
        kk = k_pages[j][...].astype(BF16)
        vv = v_pages[j][...].astype(BF16)
        s = _dot_nt(lhs, kk) + bpage_ref[p * np_ + j]
        m_prev = m_ref[...]
        m_new = jnp.maximum(m_prev, jnp.max(s, axis=-1, keepdims=True))
        alpha = jnp.exp(m_prev - m_new)
        pr = jnp.exp(s - m_new)
        l_ref[...] = alpha * l_ref[...] + jnp.sum(pr, axis=-1, keepdims=True)
        acc_ref[...] = alpha[:, 0:1] * acc_ref[...] + _dot(pr.astype(BF16), vv)
        m_ref[...] = m_new

    @pl.when(p == pl.num_programs(1) - 1)
    def _():
        o = acc_ref[...] / l_ref[:, 0:1]
        lam = lam_ref[0]
        outs = []
        for h in range(H_C):
            sl = slice(h * 128, (h + 1) * 128)
            outs.append(_diff_finish(o[2 * h:2 * h + 1, sl], o[2 * h + 1:2 * h + 2, sl], lam, dn_ref[...],
                                     out_scale))
        o_ref[0] = jnp.concatenate(outs, axis=1).astype(BF16)


def _diff_sample(page_table, lam, ck_cache, cv_cache, l, cq, ckb, cvb, bpage, nbias, dn, out_scale):
    nseq = cq.shape[0]
    np_ = PAGES_PER_STEP
    nrow = 2 * H_C

    def page_spec(j):
        return pl.BlockSpec((None, None, PAGE_SIZE, W_MIX), lambda s, p, pt, lam_: (l, pt[s, p * np_ + j], 0, 0))
    c3 = lambda s, p, pt, lam_: (s, 0, 0)
    return pl.pallas_call(
        functools.partial(_diff_sample_body, out_scale=out_scale),
        grid_spec=pltpu.PrefetchScalarGridSpec(
            num_scalar_prefetch=2,
            grid=(nseq, DEC_PAGES // np_),
            in_specs=([page_spec(j) for j in range(np_)] + [page_spec(j) for j in range(np_)]
                      + [pl.BlockSpec((1, 1, W_MIX), c3), pl.BlockSpec((1, 1, W_MIX), c3),
                         pl.BlockSpec((1, 1, W_MIX), c3),
                         pl.BlockSpec((DEC_PAGES, nrow, PAGE_SIZE), lambda s, p, pt, lam_: (0, 0, 0)),
                         pl.BlockSpec((nrow, 128), lambda s, p, pt, lam_: (0, 0)),
                         pl.BlockSpec((1, DV_C), lambda s, p, pt, lam_: (0, 0))]),
            out_specs=pl.BlockSpec((1, 1, W_MIX), c3),
            scratch_shapes=[pltpu.VMEM((nrow, 128), F32), pltpu.VMEM((nrow, 128), F32),
                            pltpu.VMEM((nrow, W_MIX), F32)]),
        out_shape=jax.ShapeDtypeStruct((nseq, 1, W_MIX), BF16),
        compiler_params=_cparams("parallel", "arbitrary"),
        name="diff_sample",
    )(page_table, lam, *([ck_cache] * np_), *([cv_cache] * np_), cq.reshape(nseq, 1, W_MIX),
      ckb.reshape(nseq, 1, W_MIX), cvb.reshape(nseq, 1, W_MIX), bpage, nbias, dn)


def _t5_bucket(dist):
    n = jnp.maximum(dist, 0)
    max_exact = N_BUCKETS // 2
    nf = jnp.maximum(n, 1).astype(F32)
    large = max_exact + (jnp.log(nf / max_exact) / math.log(REL_MAX_DIST / max_exact)
                         * (N_BUCKETS - max_exact)).astype(jnp.int32)
    return jnp.where(n < max_exact, n, jnp.minimum(large, N_BUCKETS - 1))


def _prompt_bias_tables(rel_bias):
    d = (np.arange(N_BIAS_TABLES)[:, None, None] * Q_TILE + np.arange(Q_TILE)[None, :, None]
         - np.arange(K_TILE)[None, None, :])
    vals = rel_bias.astype(F32)[_t5_bucket(jnp.asarray(d, jnp.int32))]
    vals = jnp.where(jnp.asarray(d >= 0)[..., None], vals, NEG).transpose(0, 3, 1, 2)
    tab_b = vals[:, :H_B].reshape(N_BIAS_TABLES, H_B * Q_TILE, K_TILE)
    tab_c = vals[:, H_B:]
    return tab_b, tab_c


def _prep_params(w_in, w_branch, w_out, hgrn_lb, nsa_phi_pe, nsa_phi_w1, nsa_phi_w2, diff_lambda,
                 moe_w_group, moe_b_group, moe_w_expert, moe_b_expert, moe_w1, moe_w3, moe_w2):
    p = {}
    perm = jnp.asarray(np.maximum(_IN_PERM, 0), jnp.int32)
    keep = jnp.asarray(_IN_PERM >= 0)
    p['w_in'] = jnp.where(keep, jnp.take(w_in, perm, axis=2), 0.0).astype(BF16)
    wbr = w_branch.at[:, 1].set(jnp.take(w_branch[:, 1], jnp.asarray(_NSA_OUT_PERM, jnp.int32), axis=1))
    p['w_br'] = wbr.astype(BF16)
    p['w_out'] = w_out.astype(BF16)
    lb = jnp.cumsum(jax.nn.softmax(hgrn_lb.astype(F32), axis=0), axis=0)
    lb = lb - lb[0:1]
    p['lb3'] = jnp.stack([jnp.log(lb), jnp.log1p(-lb), 1.0 - lb], axis=1)
    p['pe'] = jnp.tile(nsa_phi_pe, (1, 1, 1, G_B))
    w1r = nsa_phi_w1.reshape(DEPTH, 2, CMP_BLOCK, DH_B, CMP_HID)
    z1 = jnp.zeros_like(w1r)
    p['cw1'] = jnp.concatenate([jnp.concatenate([w1r, z1], axis=-1), jnp.concatenate([z1, w1r], axis=-1)],
                               axis=-2).astype(BF16)
    z2 = jnp.zeros_like(nsa_phi_w2)
    p['cw2'] = jnp.concatenate([jnp.concatenate([nsa_phi_w2, z2], axis=-1),
                                jnp.concatenate([z2, nsa_phi_w2], axis=-1)], axis=-2).astype(BF16)
    lp = diff_lambda.astype(F32)
    lam_init = np.array([0.8 - 0.6 * math.exp(-0.3 * l) for l in range(DEPTH)], np.float32)
    p['lam'] = (jnp.exp(jnp.sum(lp[:, 0] * lp[:, 1], axis=-1)) - jnp.exp(jnp.sum(lp[:, 2] * lp[:, 3], axis=-1))
                + jnp.asarray(lam_init))
    p['lam_init'] = lam_init
    wr = jnp.concatenate([moe_w_group, moe_w_expert,
                          jnp.zeros((DEPTH, D_MODEL, 128 - N_GROUPS - N_EXPERTS), F32)], axis=-1)
    wr_hi = wr.astype(BF16)
    p['wr'] = jnp.stack([wr_hi, (wr - wr_hi.astype(F32)).astype(BF16)], axis=1)
    p['br'] = jnp.concatenate([moe_b_group, moe_b_expert,
                               jnp.zeros((DEPTH, 128 - N_GROUPS - N_EXPERTS), F32)], axis=-1)[:, None, :]
    nf = N_EXPERTS * D_FF_EXP
    p['w1'] = moe_w1.transpose(0, 2, 1, 3).reshape(DEPTH, D_MODEL, nf).astype(BF16)
    p['w3'] = moe_w3.transpose(0, 2, 1, 3).reshape(DEPTH, D_MODEL, nf).astype(BF16)
    p['w2'] = moe_w2.reshape(DEPTH, nf, D_MODEL).astype(BF16)
    return p


def _prompt_mixer(x2d, l, p, norm_attn, hgrn_norm, diff_norm, tab_b, tab_c, bsz, t):
    (za, bq, rows, win, kv, bg, cq, ck, cv, ckb, cvb, mg) = _in_proj(x2d, norm_attn[l][None], p['w_in'][l], 256)
    oa, s_fin = _hgrn_prompt(za, p['lb3'][l], hgrn_norm[l][None], bsz, t)
    cmp = _compress_prompt(rows, p['pe'][l], p['cw1'][l], p['cw2'][l], bsz, t)
    ob = _nsa_prompt(bq, bg, cmp, kv, tab_b, bsz, t)
    oc = _diff_prompt(p['lam'][l][None], cq, ckb, cvb, tab_c, diff_norm[l][None],
                      float(1.0 - p['lam_init'][l]), bsz, t)
    x_new = _merge(x2d, oa.reshape(-1, W_MIX), ob.reshape(-1, W_MIX), oc.reshape(-1, W_MIX), mg,
                   p['w_br'][l], p['w_out'][l], 256)
    return x_new, (rows, ck, cv, win, s_fin)


def _sample_bias_tables(rel_bias):
    rb = rel_bias.astype(F32)

    def by_dist(dist, heads):
        return rb[_t5_bucket(jnp.asarray(dist, jnp.int32))][..., heads]
    hb, hc = slice(0, H_B), slice(H_B, H_B + H_C)
    d_blk = PAST_LEN - (np.arange(N_PAST_BLOCKS)[:, None] * SEL_BLOCK + np.arange(SEL_BLOCK)[None, :])
    bblk = by_dist(d_blk, hb).transpose(0, 2, 1)
    wbias = by_dist(WINDOW - np.arange(WINDOW), hb).T
    nbias_b = jnp.broadcast_to(by_dist(np.zeros((1,)), hb).T, (H_B, 128))
    d_pg = PAST_LEN - (np.arange(DEC_PAGES)[:, None] * PAGE_SIZE + np.arange(PAGE_SIZE)[None, :])
    bpage = jnp.repeat(by_dist(d_pg, hc).transpose(0, 2, 1), 2, axis=1)
    nbias_c = jnp.broadcast_to(jnp.repeat(by_dist(np.zeros((1,)), hc).T, 2, axis=0), (2 * H_C, 128))
    return bblk, wbias, nbias_b, bpage, nbias_c


def _sample_mixer(x2d, l, p, norm_attn, hgrn_norm, diff_norm, stabs, caches, win_state, hg_state, page_table):
    nseq = x2d.shape[0]
    bblk, wbias, nbias_b, bpage, nbias_c = stabs
    cache4, cache5, cache_k, cache_v = caches
    (za, bq, rows, win, kv, bg, cq, ck, cv, ckb, cvb, mg) = _in_proj(x2d, norm_attn[l][None], p['w_in'][l], nseq)
    oa, s_new = _hgrn_sample(za, hg_state, p['lb3'][l], hgrn_norm[l][None])
    o_cmp, idx = _nsa_sample_cmp(page_table, cache4, l, bq, p['pe'][l], p['cw1'][l], p['cw2'][l])
    jblk = idx[:, :G_B, :N_SEL_PAST].reshape(nseq, G_B * N_SEL_PAST)
    pages = jnp.take_along_axis(page_table, jblk // (PAGE_SIZE // SEL_BLOCK), axis=1)
    halves = jblk % (PAGE_SIZE // SEL_BLOCK)
    ob = _nsa_sample_att(pages, halves, jblk, cache5, l, bq, bg, o_cmp, kv,
                         win_state.reshape(nseq, WINDOW, 256), bblk, wbias, nbias_b)
    oc = _diff_sample(page_table, p['lam'][l][None], cache_k, cache_v, l, cq, ckb, cvb, bpage, nbias_c,
                      diff_norm[l][None], float(1.0 - p['lam_init'][l]))
    x_new = _merge(x2d, oa.reshape(-1, W_MIX), ob.reshape(-1, W_MIX), oc.reshape(-1, W_MIX), mg,
                   p['w_br'][l], p['w_out'][l], nseq)
    return x_new, (rows, ck, cv, win, s_new)


def kernel(x_prompt, x_sample, cache_nsa, cache_diff_k, cache_diff_v, state_nsa_win, state_hgrn, page_table, w_in, w_branch, w_out, norm_attn, norm_ffn, norm_final, hgrn_lb, hgrn_norm, nsa_phi_pe, nsa_phi_w1, nsa_phi_w2, diff_lambda, diff_norm, rel_bias, moe_w_group, moe_b_group, moe_w_expert, moe_b_expert, moe_w1, moe_w3, moe_w2):
    bsz, t, _ = x_prompt.shape
    nseq = x_sample.shape[0]
    assert x_sample.shape[1] == 1 and t % K_TILE == 0 and t // SEL_BLOCK <= N_SEL_LANES and t >= WINDOW
    p = _prep_params(w_in, w_branch, w_out, hgrn_lb, nsa_phi_pe, nsa_phi_w1, nsa_phi_w2, diff_lambda,
                     moe_w_group, moe_b_group, moe_w_expert, moe_b_expert, moe_w1, moe_w3, moe_w2)
    tab_b, tab_c = _prompt_bias_tables(rel_bias)
    stabs = _sample_bias_tables(rel_bias)
    n_pool = cache_nsa.shape[1]
    caches = (cache_nsa.reshape(DEPTH, n_pool, PAGE_SIZE, 512),
              cache_nsa.reshape(DEPTH, n_pool, PAGE_SIZE // SEL_BLOCK, SEL_BLOCK, 512),
              cache_diff_k.reshape(DEPTH, n_pool, PAGE_SIZE, W_MIX),
              cache_diff_v.reshape(DEPTH, n_pool, PAGE_SIZE, W_MIX))
    xp = x_prompt.reshape(bsz * t, D_MODEL)
    xs = x_sample.reshape(nseq, D_MODEL)
    st_p, st_s = [], []
    for l in range(DEPTH):
        moe_w = (norm_ffn[l][None], p['wr'][l], p['br'][l], p['w1'][l], p['w3'][l], p['w2'][l], norm_final[None])
        xp, st = _prompt_mixer(xp, l, p, norm_attn, hgrn_norm, diff_norm, tab_b, tab_c, bsz, t)
        xp, yp = _moe(xp, *moe_w, 256)
        st_p.append(st)
        xs, st = _sample_mixer(xs, l, p, norm_attn, hgrn_norm, diff_norm, stabs, caches, state_nsa_win[l],
                               state_hgrn[l], page_table)
        xs, ys = _moe(xs, *moe_w, nseq)
        st_s.append(st)

    def stack(sts, i, shape):
        return jnp.stack([s[i].reshape(shape) for s in sts])
    wb = WINDOW
    win_p = jnp.stack([s[3].reshape(bsz, t, 2, G_B, DH_B)[:, t - wb:] for s in st_p])
    win_s = jnp.stack([jnp.concatenate([state_nsa_win[l][:, 1:], st_s[l][3].reshape(nseq, 1, 2, G_B, DH_B)], axis=1)
                       for l in range(DEPTH)])
    return (yp.reshape(bsz, t, D_MODEL), ys.reshape(nseq, 1, D_MODEL),
            stack(st_p, 0, (bsz, t, 4, G_B, DH_B)), stack(st_s, 0, (nseq, 1, 4, G_B, DH_B)),
            stack(st_p, 1, (bsz, t, H_C, 2, DH_C)), stack(st_s, 1, (nseq, 1, H_C, 2, DH_C)),
            stack(st_p, 2, (bsz, t, H_C, DV_C)), stack(st_s, 2, (nseq, 1, H_C, DV_C)),
            win_p, win_s,
            stack(st_p, 4, (bsz, H_A, DK_A, DV_A)), stack(st_s, 4, (nseq, H_A, DK_A, DV_A)))
```
